```python
import math
import jax, jax.numpy as jnp
from jax import lax
import numpy as np

D_MODEL = 1024
BATCH = 8
SEQ = 4096
DEPTH = 4

GRID_W = 64
WIN_ROWS = 8
WIN_COLS = 16
NA_HEADS = 16
NA_HEAD_DIM = D_MODEL // NA_HEADS
DIFF_HEADS = 8
DIFF_HEAD_DIM = D_MODEL // (2 * DIFF_HEADS)
Q_BLOCK = 128
N_BUCKETS = 32
MAX_DISTANCE = 128
N_GROUPS = 4
EXPERTS_PER_GROUP = 4
N_EXPERTS = N_GROUPS * EXPERTS_PER_GROUP
TOP_K_INNER = 2
D_EXPERT = 512
MOE_BLOCK = 256
N_MIXERS = 2
N_NA_LAYERS = (DEPTH + 1) // 2
N_DIFF_LAYERS = DEPTH // 2
RMS_EPS = 1e-6

kernel_name = 'hybrid_natten_diffattn_hmoe_encoder'


def _rms_norm(x, g):
    xf = x.astype(jnp.float32)
    y = xf * lax.rsqrt(jnp.mean(xf * xf, axis=-1, keepdims=True) + RMS_EPS)
    return (y * g.astype(jnp.float32)).astype(x.dtype)


def _t5_bucket(rel):
    half = N_BUCKETS // 2
    max_exact = half // 2
    ret = jnp.where(rel > 0, half, 0)
    n = jnp.abs(rel)
    nf = jnp.maximum(n, 1).astype(jnp.float32)
    large = max_exact + (jnp.log(nf / max_exact) / math.log(MAX_DISTANCE / max_exact)
                         * (half - max_exact)).astype(jnp.int32)
    large = jnp.minimum(large, half - 1)
    return ret + jnp.where(n < max_exact, n, large)


def neighborhood_attention(h, w_qkv, w_o, q_g, k_g, rpb):
    b, t, d = h.shape
    rows = t // GRID_W
    kh = min(WIN_ROWS, rows)
    qkv = (h @ w_qkv).reshape(b, rows, GRID_W, 3, NA_HEADS, NA_HEAD_DIM)
    q = _rms_norm(qkv[:, :, :, 0], q_g)
    k = _rms_norm(qkv[:, :, :, 1], k_g)
    v = qkv[:, :, :, 2]
    row_start = jnp.clip(jnp.arange(rows) - kh // 2, 0, rows - kh)
    col_start = jnp.clip(jnp.arange(GRID_W) - WIN_COLS // 2, 0, GRID_W - WIN_COLS)
    col_idx = col_start[:, None] + jnp.arange(WIN_COLS)[None, :]
    col_off = col_idx - jnp.arange(GRID_W)[:, None] + (WIN_COLS - 1)
    scale = NA_HEAD_DIM ** -0.5

    def one_row(r):
        rs = row_start[r]
        q_r = lax.dynamic_index_in_dim(q, r, axis=1, keepdims=False)
        k_win = lax.dynamic_slice_in_dim(k, rs, kh, axis=1)[:, :, col_idx]
        v_win = lax.dynamic_slice_in_dim(v, rs, kh, axis=1)[:, :, col_idx]
        row_off = rs + jnp.arange(kh) - r + (WIN_ROWS - 1)
        bias = rpb[:, row_off[None, :, None], col_off[:, None, :]]
        s = (jnp.einsum('bqhd,bkqwhd->bhqkw', q_r, k_win).astype(jnp.float32) * scale
             + bias.astype(jnp.float32)[None])
        p = jax.nn.softmax(s.reshape(b, NA_HEADS, GRID_W, kh * WIN_COLS), axis=-1)
        p = p.reshape(s.shape).astype(v.dtype)
        return jnp.einsum('bhqkw,bkqwhd->bqhd', p, v_win)

    o = lax.map(one_row, jnp.arange(rows))
    o = o.transpose(1, 0, 2, 3, 4).reshape(b, t, d)
    return o @ w_o


def differential_attention(h, w_qkv, w_o, q_g, k_g, lq1, lk1, lq2, lk2, subln_g, rel_bias, lambda_init):
    b, t, d = h.shape
    qkv = h @ w_qkv
    q = _rms_norm(qkv[..., :d].reshape(b, t, DIFF_HEADS, 2, DIFF_HEAD_DIM), q_g).transpose(0, 2, 3, 1, 4)
    k = _rms_norm(qkv[..., d:2 * d].reshape(b, t, DIFF_HEADS, 2, DIFF_HEAD_DIM), k_g).transpose(0, 2, 3, 1, 4)
    v = qkv[..., 2 * d:].reshape(b, t, DIFF_HEADS, 2 * DIFF_HEAD_DIM).transpose(0, 2, 1, 3)
    f32 = jnp.float32
    lam = (jnp.exp(jnp.sum(lq1.astype(f32) * lk1.astype(f32)))
           - jnp.exp(jnp.sum(lq2.astype(f32) * lk2.astype(f32))) + lambda_init)
    k_pos = jnp.arange(t)
    scale = DIFF_HEAD_DIM ** -0.5

    def one_block(i):
        q0 = i * Q_BLOCK
        qb = lax.dynamic_slice_in_dim(q, q0, Q_BLOCK, axis=3)
        rel = k_pos[None, :] - (q0 + jnp.arange(Q_BLOCK))[:, None]
        bias = rel_bias[_t5_bucket(rel)].astype(f32).transpose(2, 0, 1)
        s = jnp.einsum('bhcqd,bhckd->bhcqk', qb, k).astype(f32) * scale + bias[None, :, None]
        p = jax.nn.softmax(s, axis=-1)
        a = (p[:, :, 0] - lam * p[:, :, 1]).astype(v.dtype)
        return jnp.einsum('bhqk,bhkd->bqhd', a, v)

    o = lax.map(one_block, jnp.arange(t // Q_BLOCK))
    o = o.transpose(1, 0, 2, 3, 4).reshape(b, t, DIFF_HEADS, 2 * DIFF_HEAD_DIM)
    o = _rms_norm(o, subln_g) * (1.0 - lambda_init)
    return o.reshape(b, t, d) @ w_o


def hierarchical_moe(h, wr_g, br_g, wr_e, br_e, w_gate, w_up, w_down):
    b, t, d = h.shape
    n_tok = b * t
    xt = h.reshape(n_tok, d)
    f32 = jnp.float32
    g_prob = jax.nn.softmax((xt @ wr_g + br_g).astype(f32), axis=-1)
    g_p, g_idx = lax.top_k(g_prob, 1)
    e_logits = (xt @ wr_e + br_e).astype(f32).reshape(n_tok, N_GROUPS, EXPERTS_PER_GROUP)
    e_in = jnp.take_along_axis(e_logits, g_idx[:, :, None], axis=1)[:, 0]
    e_l, e_idx = lax.top_k(e_in, TOP_K_INNER)
    gate = jax.nn.softmax(e_l, axis=-1) * g_p
    expert = g_idx * EXPERTS_PER_GROUP + e_idx
    n_assign = n_tok * TOP_K_INNER
    flat_e = expert.reshape(n_assign)
    flat_tok = jnp.repeat(jnp.arange(n_tok, dtype=jnp.int32), TOP_K_INNER)
    flat_w = gate.reshape(n_assign)
    order = jnp.argsort(flat_e)
    sorted_e = flat_e[order]
    counts = jnp.bincount(flat_e, length=N_EXPERTS)
    starts = jnp.cumsum(counts) - counts
    padded = (counts + MOE_BLOCK - 1) // MOE_BLOCK * MOE_BLOCK
    padded_end = jnp.cumsum(padded)
    padded_start = padded_end - padded
    dest = padded_start[sorted_e] + jnp.arange(n_assign) - starts[sorted_e]
    n_blocks = -(-n_assign // MOE_BLOCK) + N_EXPERTS
    cap = n_blocks * MOE_BLOCK
    slot_tok = jnp.zeros((cap,), jnp.int32).at[dest].set(flat_tok[order])
    slot_w = jnp.zeros((cap,), f32).at[dest].set(flat_w[order])
    block_expert = jnp.minimum(
        jnp.searchsorted(padded_end, jnp.arange(n_blocks) * MOE_BLOCK, side='right'), N_EXPERTS - 1)
    xs = xt[slot_tok].reshape(n_blocks, MOE_BLOCK, d)

    def expert_block(args):
        xb, e = args
        hid = jax.nn.silu(xb @ w_gate[e]) * (xb @ w_up[e])
        return hid @ w_down[e]

    ys = lax.map(expert_block, (xs, block_expert)).reshape(cap, d)
    y = jnp.zeros_like(xt).at[slot_tok].add(ys * slot_w[:, None].astype(ys.dtype))
    return y.reshape(b, t, d)


def setup_inputs(seed: int = 0) -> dict:
    key = jax.random.key(seed)
    ks = jax.random.split(key, 26)
    f32 = jnp.float32
    D = D_MODEL

    def nrm(k, shape, scale):
        return jax.random.normal(k, shape, f32) * scale

    return {
        'x': nrm(ks[0], (BATCH, SEQ, D), 1.0),
        'mix_norm_g': 1.0 + nrm(ks[1], (DEPTH, D), 0.02),
        'ffn_norm_g': 1.0 + nrm(ks[2], (DEPTH, D), 0.02),
        'na_w_qkv': nrm(ks[3], (N_NA_LAYERS, D, 3 * D), D ** -0.5),
        'na_w_o': nrm(ks[4], (N_NA_LAYERS, D, D), D ** -0.5),
        'na_q_g': 1.0 + nrm(ks[5], (N_NA_LAYERS, NA_HEAD_DIM), 0.02),
        'na_k_g': 1.0 + nrm(ks[6], (N_NA_LAYERS, NA_HEAD_DIM), 0.02),
        'na_rpb': nrm(ks[7], (N_NA_LAYERS, NA_HEADS, 2 * WIN_ROWS - 1, 2 * WIN_COLS - 1), 0.1),
        'diff_w_qkv': nrm(ks[8], (N_DIFF_LAYERS, D, 3 * D), D ** -0.5),
        'diff_w_o': nrm(ks[9], (N_DIFF_LAYERS, D, D), D ** -0.5),
        'diff_q_g': 1.0 + nrm(ks[10], (N_DIFF_LAYERS, DIFF_HEAD_DIM), 0.02),
        'diff_k_g': 1.0 + nrm(ks[11], (N_DIFF_LAYERS, DIFF_HEAD_DIM), 0.02),
        'diff_lq1': nrm(ks[12], (N_DIFF_LAYERS, DIFF_HEAD_DIM), 0.1),
        'diff_lk1': nrm(ks[13], (N_DIFF_LAYERS, DIFF_HEAD_DIM), 0.1),
        'diff_lq2': nrm(ks[14], (N_DIFF_LAYERS, DIFF_HEAD_DIM), 0.1),
        'diff_lk2': nrm(ks[15], (N_DIFF_LAYERS, DIFF_HEAD_DIM), 0.1),
        'diff_subln_g': 1.0 + nrm(ks[16], (N_DIFF_LAYERS, 2 * DIFF_HEAD_DIM), 0.02),
        'rel_bias': nrm(ks[17], (N_BUCKETS, DIFF_HEADS), 0.1),
        'router_group_w': nrm(ks[18], (DEPTH, D, N_GROUPS), D ** -0.5),
        'router_group_b': nrm(ks[19], (DEPTH, N_GROUPS), 0.01),
        'router_expert_w': nrm(ks[20], (DEPTH, D, N_EXPERTS), D ** -0.5),
        'router_expert_b': nrm(ks[21], (DEPTH, N_EXPERTS), 0.01),
        'moe_w_gate': nrm(ks[22], (DEPTH, N_EXPERTS, D, D_EXPERT), D ** -0.5),
        'moe_w_up': nrm(ks[23], (DEPTH, N_EXPERTS, D, D_EXPERT), D ** -0.5),
        'moe_w_down': nrm(ks[24], (DEPTH, N_EXPERTS, D_EXPERT, D), D_EXPERT ** -0.5),
    }


def reference(x, mix_norm_g, ffn_norm_g, na_w_qkv, na_w_o, na_q_g, na_k_g, na_rpb,
              diff_w_qkv, diff_w_o, diff_q_g, diff_k_g, diff_lq1, diff_lk1, diff_lq2, diff_lk2,
              diff_subln_g, rel_bias, router_group_w, router_group_b, router_expert_w,
              router_expert_b, moe_w_gate, moe_w_up, moe_w_down):
    for i in range(DEPTH):
        j = i // N_MIXERS
        h = _rms_norm(x, mix_norm_g[i])
        if i % N_MIXERS == 0:
            x = x + neighborhood_attention(h, na_w_qkv[j], na_w_o[j], na_q_g[j], na_k_g[j], na_rpb[j])
        else:
            lambda_init = 0.8 - 0.6 * math.exp(-0.3 * i)
            x = x + differential_attention(h, diff_w_qkv[j], diff_w_o[j], diff_q_g[j], diff_k_g[j],
                                           diff_lq1[j], diff_lk1[j], diff_lq2[j], diff_lk2[j],
                                           diff_subln_g[j], rel_bias, lambda_init)
        h = _rms_norm(x, ffn_norm_g[i])
        x = x + hierarchical_moe(h, router_group_w[i], router_group_b[i], router_expert_w[i],
                                 router_expert_b[i], moe_w_gate[i], moe_w_up[i], moe_w_down[i])
    return x
```

```python
import functools
import math

import jax
import jax.numpy as jnp
import numpy as np
from jax import lax
from jax.experimental import pallas as pl
from jax.experimental.pallas import tpu as pltpu

F32 = jnp.float32
BF16 = jnp.bfloat16

D_MODEL = 1024
GRID_W = 64
WIN_ROWS = 8
WIN_COLS = 16
NA_HEADS = 16
NA_HEAD_DIM = 64
DIFF_HEADS = 8
DIFF_HEAD_DIM = 64
N_BUCKETS = 32
MAX_DISTANCE = 128
N_GROUPS = 4
EXPERTS_PER_GROUP = 4
N_EXPERTS = 16
TOP_K = 2
D_EXPERT = 512
MOE_BLOCK = 256
RMS_EPS = 1e-6
NEG_BIG = -1e30

LANES = 128
VMEM_LIMIT = 56 << 20

TOK_TILE = 512
NA_GROUP_ROWS = 4
NA_WIN_ROWS = NA_GROUP_ROWS + WIN_ROWS
DIFF_TQ = 256
DIFF_TK = 512
ROUTER_LANES = 128


def _cparams(sem):
    return pltpu.CompilerParams(dimension_semantics=sem, vmem_limit_bytes=VMEM_LIMIT)


def _rms(x, g):
    return x * lax.rsqrt(jnp.mean(x * x, axis=-1, keepdims=True) + RMS_EPS) * g


def _half_rms(x, g2):
    lo = lax.broadcasted_iota(jnp.int32, x.shape, 1) < NA_HEAD_DIM
    sq = x * x
    s_lo = jnp.sum(jnp.where(lo, sq, 0.0), axis=-1, keepdims=True)
    s_hi = jnp.sum(jnp.where(lo, 0.0, sq), axis=-1, keepdims=True)
    r = jnp.where(lo, lax.rsqrt(s_lo / NA_HEAD_DIM + RMS_EPS), lax.rsqrt(s_hi / NA_HEAD_DIM + RMS_EPS))
    return x * r * g2


def _qkv_kernel(*refs, combine):
    if combine:
        x_ref, ys_ref, r_ref, g_ref, w_ref, xo_ref, qkv_ref = refs
        x = x_ref[...] + r_ref[:, 2:3] * ys_ref[:, :D_MODEL] + r_ref[:, 3:4] * ys_ref[:, D_MODEL:]
        xo_ref[...] = x
    else:
        x_ref, g_ref, w_ref, qkv_ref = refs
        x = x_ref[...]
    h = _rms(x, g_ref[...]).astype(BF16)
    for c in range(3):
        cols = slice(c * D_MODEL, (c + 1) * D_MODEL)
        qkv_ref[:, cols] = jnp.dot(h, w_ref[:, cols], preferred_element_type=F32)


def _qkv_call(x, g, w_bf16, moe=None):
    n, d = x.shape
    tm = TOK_TILE
    row = lambda i: (i, 0)
    fixed = lambda i: (0, 0)
    qkv_shape = jax.ShapeDtypeStruct((n, 3 * d), F32)
    qkv_spec = pl.BlockSpec((tm, 3 * d), row)
    tail_specs = [pl.BlockSpec((1, d), fixed), pl.BlockSpec((d, 3 * d), fixed)]
    if moe is None:
        return None, pl.pallas_call(
            functools.partial(_qkv_kernel, combine=False),
            grid=(n // tm,),
            in_specs=[pl.BlockSpec((tm, d), row)] + tail_specs,
            out_specs=qkv_spec,
            out_shape=qkv_shape,
            compiler_params=_cparams(("arbitrary",)),
            name="norm_qkv",
        )(x, g, w_bf16)
    ys2, route = moe
    return pl.pallas_call(
        functools.partial(_qkv_kernel, combine=True),
        grid=(n // tm,),
        in_specs=[pl.BlockSpec((tm, d), row), pl.BlockSpec((tm, 2 * d), row),
                  pl.BlockSpec((tm, ROUTER_LANES), row)] + tail_specs,
        out_specs=[pl.BlockSpec((tm, d), row), qkv_spec],
        out_shape=[jax.ShapeDtypeStruct((n, d), F32), qkv_shape],
        compiler_params=_cparams(("arbitrary",)),
        name="combine_norm_qkv",
    )(x, ys2, route, g, w_bf16)


def _combine_kernel(x_ref, ys_ref, r_ref, o_ref):
    o_ref[...] = x_ref[...] + r_ref[:, 2:3] * ys_ref[:, :D_MODEL] + r_ref[:, 3:4] * ys_ref[:, D_MODEL:]


def _combine_call(x, ys2, route):
    n, d = x.shape
    tm = TOK_TILE
    row = lambda i: (i, 0)
    return pl.pallas_call(
        _combine_kernel,
        grid=(n // tm,),
        in_specs=[pl.BlockSpec((tm, d), row), pl.BlockSpec((tm, 2 * d), row),
                  pl.BlockSpec((tm, ROUTER_LANES), row)],
        out_specs=pl.BlockSpec((tm, d), row),
        out_shape=jax.ShapeDtypeStruct((n, d), F32),
        compiler_params=_cparams(("arbitrary",)),
        name="combine_final",
    )(x, ys2, route)


def _proj_router_kernel(x_ref, o_ref, wo_ref, g_ref, wr_ref, br_ref, xo_ref, h_ref, r_ref):
    x = x_ref[...] + jnp.dot(o_ref[...], wo_ref[...], preferred_element_type=F32)
    xo_ref[...] = x
    h = _rms(x, g_ref[...])
    h_ref[...] = h
    logits = jnp.dot(h, wr_ref[...], preferred_element_type=F32,
                     precision=lax.Precision.HIGHEST) + br_ref[...]
    lane = lax.broadcasted_iota(jnp.int32, logits.shape, 1)
    last = ROUTER_LANES - 1

    def first_argmax(v, vmax):
        return jnp.min(jnp.where(v == vmax, lane, last), axis=-1, keepdims=True)

    gl = jnp.where(lane < N_GROUPS, logits, -jnp.inf)
    gmax = jnp.max(gl, axis=-1, keepdims=True)
    gsum = jnp.sum(jnp.exp(gl - gmax), axis=-1, keepdims=True)
    g_p = 1.0 / gsum
    g_idx = first_argmax(gl, gmax)
    e_lo = N_GROUPS + EXPERTS_PER_GROUP * g_idx
    el = jnp.where((lane >= e_lo) & (lane < e_lo + EXPERTS_PER_GROUP), logits, -jnp.inf)
    m1 = jnp.max(el, axis=-1, keepdims=True)
    i1 = first_argmax(el, m1)
    el2 = jnp.where(lane == i1, -jnp.inf, el)
    m2 = jnp.max(el2, axis=-1, keepdims=True)
    i2 = first_argmax(el2, m2)
    e2 = jnp.exp(m2 - m1)
    gate1 = g_p / (1.0 + e2)
    gate2 = g_p * e2 / (1.0 + e2)
    out = jnp.where(lane == 0, (i1 - N_GROUPS).astype(F32), 0.0)
    out = jnp.where(lane == 1, (i2 - N_GROUPS).astype(F32), out)
    out = jnp.where(lane == 2, gate1, out)
    out = jnp.where(lane == 3, gate2, out)
    r_ref[...] = out


def _proj_router_call(x, o_bf16, wo_bf16, g, wr, br):
    n, d = x.shape
    tm = TOK_TILE
    row = lambda i: (i, 0)
    fixed = lambda i: (0, 0)
    return pl.pallas_call(
        _proj_router_kernel,
        grid=(n // tm,),
        in_specs=[pl.BlockSpec((tm, d), row), pl.BlockSpec((tm, d), row), pl.BlockSpec((d, d), fixed),
                  pl.BlockSpec((1, d), fixed), pl.BlockSpec((d, ROUTER_LANES), fixed),
                  pl.BlockSpec((1, ROUTER_LANES), fixed)],
        out_specs=[pl.BlockSpec((tm, d), row), pl.BlockSpec((tm, d), row),
                   pl.BlockSpec((tm, ROUTER_LANES), row)],
        out_shape=[jax.ShapeDtypeStruct((n, d), F32), jax.ShapeDtypeStruct((n, d), F32),
                   jax.ShapeDtypeStruct((n, ROUTER_LANES), F32)],
        compiler_params=_cparams(("arbitrary",)),
        name="proj_router",
    )(x, o_bf16, wo_bf16, g, wr, br)


def _na_bias_table(rpb):
    rows = 4096 // GRID_W
    tiles = []
    for r0, ws in ((0, 0), (NA_GROUP_ROWS, 0), (rows - NA_GROUP_ROWS, rows - NA_WIN_ROWS)):
        i = np.arange(NA_GROUP_ROWS)[:, None, None, None]
        c = np.arange(GRID_W)[None, :, None, None]
        krl = np.arange(NA_WIN_ROWS)[None, None, :, None]
        kc = np.arange(GRID_W)[None, None, None, :]
        r = r0 + i
        kr = ws + krl
        rs = np.clip(r - WIN_ROWS // 2, 0, rows - WIN_ROWS)
        cs = np.clip(c - WIN_COLS // 2, 0, GRID_W - WIN_COLS)
        valid = (kr >= rs) & (kr < rs + WIN_ROWS) & (kc >= cs) & (kc < cs + WIN_COLS)
        ro = np.broadcast_to(np.clip(kr - r + WIN_ROWS - 1, 0, 2 * WIN_ROWS - 2), valid.shape)
        co = np.broadcast_to(np.clip(kc - c + WIN_COLS - 1, 0, 2 * WIN_COLS - 2), valid.shape)
        vals = rpb[:, ro, co]
        tile = jnp.where(valid[None], vals, NEG_BIG)
        tiles.append(tile.reshape(NA_HEADS, NA_GROUP_ROWS * GRID_W, NA_WIN_ROWS * GRID_W))
    return jnp.stack(tiles, axis=1)


def _na_kernel(q_ref, k_ref, v_ref, qg_ref, kg_ref, bias_ref, o_ref, qn, kn, vb):
    t = q_ref.shape[1]
    rows = t // GRID_W
    n_groups = rows // NA_GROUP_ROWS
    gq = NA_GROUP_ROWS * GRID_W
    gk = NA_WIN_ROWS * GRID_W
    scale = NA_HEAD_DIM ** -0.5
    prep_rows = 512

    def prep(c, carry):
        sl = pl.ds(pl.multiple_of(c * prep_rows, prep_rows), prep_rows)
        qn[sl, :] = (_half_rms(q_ref[0, sl, :], qg_ref[...]) * scale).astype(BF16)
        kn[sl, :] = _half_rms(k_ref[0, sl, :], kg_ref[...]).astype(BF16)
        vb[sl, :] = v_ref[0, sl, :].astype(BF16)
        return carry

    lax.fori_loop(0, t // prep_rows, prep, 0)

    def group(g, carry):
        r0 = g * NA_GROUP_ROWS
        ws = jnp.clip(r0 - WIN_ROWS // 2, 0, rows - NA_WIN_ROWS)
        pat = jnp.where(g == 0, 0, jnp.where(g == n_groups - 1, 2, 1))
        qs = pl.ds(pl.multiple_of(r0 * GRID_W, GRID_W), gq)
        ks = pl.ds(pl.multiple_of(ws * GRID_W, GRID_W), gk)
        qg = qn[qs, :]
        kw = kn[ks, :]
        vw = vb[ks, :]
        lo = lax.broadcasted_iota(jnp.int32, qg.shape, 1) < NA_HEAD_DIM
        outs = []
        for hh in range(2):
            qh = jnp.where(lo if hh == 0 else jnp.logical_not(lo), qg, jnp.zeros_like(qg))
            s = lax.dot_general(qh, kw, (((1,), (1,)), ((), ())), preferred_element_type=F32)
            s = s + bias_ref[hh, pat]
            m = jnp.max(s, axis=-1, keepdims=True)
            p = jnp.exp(s - m)
            l = jnp.sum(p, axis=-1, keepdims=True)
            outs.append(jnp.dot(p.astype(BF16), vw, preferred_element_type=F32) / l)
        o_ref[0, qs, :] = jnp.where(lo, outs[0], outs[1]).astype(o_ref.dtype)
        return carry

    lax.fori_loop(0, n_groups, group, 0)


def _na_call(qkv, q_g, k_g, bias_tab):
    b, t, _ = qkv.shape
    n_pairs = NA_HEADS // 2
    g2 = lambda g: jnp.concatenate([g, g]).reshape(1, LANES)
    gq = NA_GROUP_ROWS * GRID_W
    gk = NA_WIN_ROWS * GRID_W
    blk = lambda off: pl.BlockSpec((1, t, LANES), lambda hp, bb, off=off: (bb, 0, off + hp))
    return pl.pallas_call(
        _na_kernel,
        grid=(n_pairs, b),
        in_specs=[blk(0), blk(n_pairs), blk(2 * n_pairs),
                  pl.BlockSpec((1, LANES), lambda hp, bb: (0, 0)),
                  pl.BlockSpec((1, LANES), lambda hp, bb: (0, 0)),
                  pl.BlockSpec((2, 3, gq, gk), lambda hp, bb: (hp, 0, 0, 0))],
        out_specs=pl.BlockSpec((1, t, LANES), lambda hp, bb: (bb, 0, hp)),
        out_shape=jax.ShapeDtypeStruct((b, t, D_MODEL), BF16),
        scratch_shapes=[pltpu.VMEM((t, LANES), BF16)] * 3,
        compiler_params=_cparams(("arbitrary", "arbitrary")),
        name="neighborhood_attention",
    )(qkv, qkv, qkv, g2(q_g), g2(k_g), bias_tab)


def _t5_bucket(rel):
    half = N_BUCKETS // 2
    max_exact = half // 2
    ret = jnp.where(rel > 0, half, 0)
    n = jnp.abs(rel)
    nf = jnp.maximum(n, 1).astype(jnp.float32)
    large = max_exact + (jnp.log(nf / max_exact) / math.log(MAX_DISTANCE / max_exact)
                         * (half - max_exact)).astype(jnp.int32)
    large = jnp.minimum(large, half - 1)
    return ret + jnp.where(n < max_exact, n, large)


_DIFF_DELTAS = (-2 * DIFF_TK, -DIFF_TK, -DIFF_TQ, 0, DIFF_TQ, 2 * DIFF_TK)


def _diff_bias_table(rel_bias):
    a = np.arange(DIFF_TQ)[:, None]
    kk = np.arange(DIFF_TK)[None, :]
    rel = jnp.asarray(np.stack([d + kk - a for d in _DIFF_DELTAS]).astype(np.int32))
    return rel_bias[_t5_bucket(rel)].astype(F32).transpose(3, 0, 1, 2)


def _diff_kernel(q_ref, k_ref, v_ref, qg_ref, kg_ref, sg_ref, lam_ref, bias_ref, o_ref, kn, vp, s_buf,
                 *, lambda_init):
    t = k_ref.shape[1]
    tq = DIFF_TQ
    tk = DIFF_TK
    n_chunks = t // tk
    qi = pl.program_id(2)
    scale = DIFF_HEAD_DIM ** -0.5

    @pl.when(qi == 0)
    def _():
        prep_rows = 512
        ones_col = (lax.broadcasted_iota(jnp.int32, (prep_rows, LANES), 1) == 0).astype(BF16)

        def prep(c, carry):
            sl = pl.ds(pl.multiple_of(c * prep_rows, prep_rows), prep_rows)
            kn[sl, :] = _half_rms(k_ref[0, sl, :], kg_ref[...]).astype(BF16)
            vp[sl, :LANES] = v_ref[0, sl, :].astype(BF16)
            vp[sl, LANES:] = ones_col
            return carry

        lax.fori_loop(0, t // prep_rows, prep, 0)

    q = _half_rms(q_ref[0], qg_ref[...]) * scale
    lo = lax.broadcasted_iota(jnp.int32, q.shape, 1) < DIFF_HEAD_DIM
    q2 = jnp.concatenate([jnp.where(lo, q, 0.0), jnp.where(lo, 0.0, q)], axis=0).astype(BF16)

    mpart = jnp.full((2 * tq, LANES), -jnp.inf, F32)
    for j in range(n_chunks):
        s = lax.dot_general(q2, kn[j * tk:(j + 1) * tk, :], (((1,), (1,)), ((), ())),
                            preferred_element_type=F32)
        var = jnp.clip(2 * j - qi + 3, 0, len(_DIFF_DELTAS) - 1)
        bt = bias_ref[0, var]
        s = s + jnp.concatenate([bt, bt], axis=0)
        s_buf[:, j * tk:(j + 1) * tk] = s
        for c in range(tk // LANES):
            mpart = jnp.maximum(mpart, s[:, c * LANES:(c + 1) * LANES])
    m = jnp.max(mpart, axis=-1, keepdims=True)

    acc = jnp.zeros((2 * tq, 2 * LANES), F32)
    for j in range(n_chunks):
        p = jnp.exp(s_buf[:, j * tk:(j + 1) * tk] - m).astype(BF16)
        acc = acc + jnp.dot(p, vp[j * tk:(j + 1) * tk, :], preferred_element_type=F32)

    lam = lam_ref[0, 0]
    o0 = acc[:tq, :LANES] / acc[:tq, LANES:LANES + 1]
    o1 = acc[tq:, :LANES] / acc[tq:, LANES:LANES + 1]
    o = o0 - lam * o1
    o = _rms(o, sg_ref[...]) * (1.0 - lambda_init)
    o_ref[0] = o.astype(o_ref.dtype)


def _diff_call(qkv, q_g, k_g, subln_g, lam, bias_tab, lambda_init):
    b, t, _ = qkv.shape
    g2 = lambda g: jnp.concatenate([g, g]).reshape(1, LANES)
    fixed = lambda bb, h, qi: (0, 0)
    return pl.pallas_call(
        functools.partial(_diff_kernel, lambda_init=lambda_init),
        grid=(b, DIFF_HEADS, t // DIFF_TQ),
        in_specs=[pl.BlockSpec((1, DIFF_TQ, LANES), lambda bb, h, qi: (bb, qi, h)),
                  pl.BlockSpec((1, t, LANES), lambda bb, h, qi: (bb, 0, DIFF_HEADS + h)),
                  pl.BlockSpec((1, t, LANES), lambda bb, h, qi: (bb, 0, 2 * DIFF_HEADS + h)),
                  pl.BlockSpec((1, LANES), fixed), pl.BlockSpec((1, LANES), fixed),
                  pl.BlockSpec((1, LANES), fixed),
                  pl.BlockSpec(memory_space=pltpu.SMEM),
                  pl.BlockSpec((1, len(_DIFF_DELTAS), DIFF_TQ, DIFF_TK), lambda bb, h, qi: (h, 0, 0, 0))],
        out_specs=pl.BlockSpec((1, DIFF_TQ, LANES), lambda bb, h, qi: (bb, qi, h)),
        out_shape=jax.ShapeDtypeStruct((b, t, D_MODEL), BF16),
        scratch_shapes=[pltpu.VMEM((t, LANES), BF16), pltpu.VMEM((t, 2 * LANES), BF16),
                        pltpu.VMEM((2 * DIFF_TQ, t), F32)],
        compiler_params=_cparams(("arbitrary", "arbitrary", "arbitrary")),
        name="differential_attention",
    )(qkv, qkv, qkv, g2(q_g), g2(k_g), subln_g.reshape(1, LANES), lam, bias_tab)


def _gather_kernel(idx_ref, src_ref, out_ref, sem):
    rows = out_ref.shape[0]
    base = pl.program_id(0) * rows

    def row_copy(i, src_row):
        return pltpu.make_async_copy(src_ref.at[pl.ds(src_row, 1)], out_ref.at[pl.ds(i, 1)], sem)

    def issue(i, carry):
        row_copy(i, idx_ref[base + i]).start()
        return carry

    lax.fori_loop(0, rows, issue, 0)

    def wait(i, carry):
        row_copy(i, 0).wait()
        return carry

    lax.fori_loop(0, rows, wait, 0)


def _gather_rows(src, idx):
    m = idx.shape[0]
    c = src.shape[1]
    return pl.pallas_call(
        _gather_kernel,
        grid_spec=pltpu.PrefetchScalarGridSpec(
            num_scalar_prefetch=1,
            grid=(m // MOE_BLOCK,),
            in_specs=[pl.BlockSpec(memory_space=pl.ANY)],
            out_specs=pl.BlockSpec((MOE_BLOCK, c), lambda i, idx: (i, 0)),
            scratch_shapes=[pltpu.SemaphoreType.DMA(())],
        ),
        out_shape=jax.ShapeDtypeStruct((m, c), src.dtype),
        compiler_params=_cparams(("arbitrary",)),
        name="gather_rows",
    )(idx, src)


def _expert_kernel(be_ref, xs_ref, wg_ref, wu_ref, wd_ref, ys_ref, wg_b, wu_b, wd_b):
    i = pl.program_id(0)

    @pl.when((i == 0) | (be_ref[i] != be_ref[jnp.maximum(i - 1, 0)]))
    def _():
        wg_b[...] = wg_ref[0].astype(BF16)
        wu_b[...] = wu_ref[0].astype(BF16)
        wd_b[...] = wd_ref[0].astype(BF16)

    xb = xs_ref[...].astype(BF16)
    gate = jnp.dot(xb, wg_b[...], preferred_element_type=F32)
    up = jnp.dot(xb, wu_b[...], preferred_element_type=F32)
    hid = (gate * (1.0 / (1.0 + jnp.exp(-gate))) * up).astype(BF16)
    ys_ref[...] = jnp.dot(hid, wd_b[...], preferred_element_type=F32)


def _expert_call(xs, block_expert, w_gate, w_up, w_down):
    cap, d = xs.shape
    return pl.pallas_call(
        _expert_kernel,
        grid_spec=pltpu.PrefetchScalarGridSpec(
            num_scalar_prefetch=1,
            grid=(cap // MOE_BLOCK,),
            in_specs=[pl.BlockSpec((MOE_BLOCK, d), lambda i, be: (i, 0)),
                      pl.BlockSpec((1, d, D_EXPERT), lambda i, be: (be[i], 0, 0)),
                      pl.BlockSpec((1, d, D_EXPERT), lambda i, be: (be[i], 0, 0)),
                      pl.BlockSpec((1, D_EXPERT, d), lambda i, be: (be[i], 0, 0))],
            out_specs=pl.BlockSpec((MOE_BLOCK, d), lambda i, be: (i, 0)),
            scratch_shapes=[pltpu.VMEM((d, D_EXPERT), BF16), pltpu.VMEM((d, D_EXPERT), BF16),
                            pltpu.VMEM((D_EXPERT, d), BF16)],
        ),
        out_shape=jax.ShapeDtypeStruct((cap, d), F32),
        compiler_params=_cparams(("arbitrary",)),
        name="expert_swiglu",
    )(block_expert, xs, w_gate, w_up, w_down)


def _dispatch_plan(route, n_tok):
    n_assign = n_tok * TOP_K
    n_blocks = -(-n_assign // MOE_BLOCK) + N_EXPERTS
    cap = n_blocks * MOE_BLOCK
    flat_e = route[:, :TOP_K].astype(jnp.int32).reshape(n_assign)
    onehot = (flat_e[:, None] == jnp.arange(N_EXPERTS, dtype=jnp.int32)[None, :]).astype(jnp.int32)
    csum = jnp.cumsum(onehot, axis=0)
    counts = csum[-1]
    padded = (counts + MOE_BLOCK - 1) // MOE_BLOCK * MOE_BLOCK
    padded_end = jnp.cumsum(padded)
    padded_start = padded_end - padded
    dest = jnp.sum(onehot * (csum - 1 + padded_start[None, :]), axis=1)
    flat_tok = jnp.arange(n_assign, dtype=jnp.int32) // TOP_K
    slot_tok = (jnp.arange(cap, dtype=jnp.int32) % n_tok).at[dest].set(flat_tok)
    block_expert = jnp.minimum(
        jnp.searchsorted(padded_end, jnp.arange(n_blocks, dtype=jnp.int32) * MOE_BLOCK, side='right'),
        N_EXPERTS - 1).astype(jnp.int32)
    return slot_tok, dest.astype(jnp.int32), block_expert


def _moe_call(h, route, w_gate, w_up, w_down):
    n_tok, d = h.shape
    slot_tok, dest, block_expert = _dispatch_plan(route, n_tok)
    xs = _gather_rows(h, slot_tok)
    ys = _expert_call(xs, block_expert, w_gate, w_up, w_down)
    return _gather_rows(ys, dest).reshape(n_tok, TOP_K * d)


def kernel(x, mix_norm_g, ffn_norm_g, na_w_qkv, na_w_o, na_q_g, na_k_g, na_rpb, diff_w_qkv, diff_w_o, diff_q_g, diff_k_g, diff_lq1, diff_lk1, diff_lq2, diff_lk2, diff_subln_g, rel_bias, router_group_w, router_group_b, router_expert_w, router_expert_b, moe_w_gate, moe_w_up, moe_w_down):
    b, t, d = x.shape
    depth = mix_norm_g.shape[0]
    n_tok = b * t
    xt = x.reshape(n_tok, d)
    diff_bias = _diff_bias_table(rel_bias)
    pad = ROUTER_LANES - N_GROUPS - N_EXPERTS
    moe = None
    for i in range(depth):
        j = i // 2
        is_na = i % 2 == 0
        w_qkv = (na_w_qkv if is_na else diff_w_qkv)[j].astype(BF16)
        w_o = (na_w_o if is_na else diff_w_o)[j].astype(BF16)
        x_new, qkv = _qkv_call(xt, mix_norm_g[i].reshape(1, d), w_qkv, moe)
        if x_new is not None:
            xt = x_new
        qkv = qkv.reshape(b, t, 3 * d)
        if is_na:
            o = _na_call(qkv, na_q_g[j], na_k_g[j], _na_bias_table(na_rpb[j]))
        else:
            lambda_init = 0.8 - 0.6 * math.exp(-0.3 * i)
            lam = (jnp.exp(jnp.sum(diff_lq1[j] * diff_lk1[j])) - jnp.exp(jnp.sum(diff_lq2[j] * diff_lk2[j]))
                   + lambda_init).reshape(1, 1)
            o = _diff_call(qkv, diff_q_g[j], diff_k_g[j], diff_subln_g[j], lam, diff_bias, lambda_init)
        wr = jnp.pad(jnp.concatenate([router_group_w[i], router_expert_w[i]], axis=1), ((0, 0), (0, pad)))
        br = jnp.pad(jnp.concatenate([router_group_b[i], router_expert_b[i]]), (0, pad)).reshape(1, ROUTER_LANES)
        xt, h, route = _proj_router_call(xt, o.reshape(n_tok, d), w_o, ffn_norm_g[i].reshape(1, d), wr, br)
        moe = (_moe_call(h, route, moe_w_gate[i], moe_w_up[i], moe_w_down[i]), route)
    return _combine_call(xt, *moe).reshape(b, t, d)
```

```python
import functools
import math

import jax
import jax.numpy as jnp
import numpy as np
from jax import lax
from jax.experimental import pallas as pl
from jax.experimental.pallas import tpu as pltpu
from jax.experimental.pallas import tpu_sc as plsc

F32 = jnp.float32
BF16 = jnp.bfloat16

D_MODEL = 1024
GRID_W = 64
WIN_ROWS = 8
WIN_COLS = 16
NA_HEADS = 16
NA_HEAD_DIM = 64
DIFF_HEADS = 8
DIFF_HEAD_DIM = 64
N_BUCKETS = 32
MAX_DISTANCE = 128
N_GROUPS = 4
EXPERTS_PER_GROUP = 4
N_EXPERTS = 16
TOP_K = 2
D_EXPERT = 512
MOE_BLOCK = 256
RMS_EPS = 1e-6
NEG_BIG = -1e30

LANES = 128
VMEM_LIMIT = 56 << 20

TOK_TILE = 512
NA_GROUP_ROWS = 4
NA_WIN_ROWS = NA_GROUP_ROWS + WIN_ROWS
DIFF_TQ = 256
DIFF_TK = 512
ROUTER_LANES = 128


def _cparams(sem):
    return pltpu.CompilerParams(dimension_semantics=sem, vmem_limit_bytes=VMEM_LIMIT)


def _rms(x, g):
    return x * lax.rsqrt(jnp.mean(x * x, axis=-1, keepdims=True) + RMS_EPS) * g


def _half_rms(x, g2):
    lo = lax.broadcasted_iota(jnp.int32, x.shape, 1) < NA_HEAD_DIM
    sq = x * x
    s_lo = jnp.sum(jnp.where(lo, sq, 0.0), axis=-1, keepdims=True)
    s_hi = jnp.sum(jnp.where(lo, 0.0, sq), axis=-1, keepdims=True)
    r = jnp.where(lo, lax.rsqrt(s_lo / NA_HEAD_DIM + RMS_EPS), lax.rsqrt(s_hi / NA_HEAD_DIM + RMS_EPS))
    return x * r * g2


def _moe_combine(x_ref, ya_ref, yb_ref, r_ref):
    return x_ref[...] + r_ref[:, 2:3] * ya_ref[...] + r_ref[:, 3:4] * yb_ref[...]


def _combine_specs(n, d, tm):
    row = lambda i: (i, 0)
    return [pl.BlockSpec((tm, d), row), pl.BlockSpec((tm, d), row),
            pl.BlockSpec((tm, d), lambda i: (i + n // tm, 0)), pl.BlockSpec((tm, ROUTER_LANES), row)]


def _qkv_kernel(*refs, combine):
    if combine:
        x_ref, ya_ref, yb_ref, r_ref, g_ref, w_ref, xo_ref, qkv_ref = refs
        x = _moe_combine(x_ref, ya_ref, yb_ref, r_ref)
        xo_ref[...] = x
    else:
        x_ref, g_ref, w_ref, qkv_ref = refs
        x = x_ref[...]
    h = _rms(x, g_ref[...]).astype(BF16)
    for c in range(3):
        cols = slice(c * D_MODEL, (c + 1) * D_MODEL)
        qkv_ref[:, cols] = jnp.dot(h, w_ref[:, cols], preferred_element_type=F32)


def _qkv_call(x, g, w_bf16, moe=None):
    n, d = x.shape
    tm = TOK_TILE
    row = lambda i: (i, 0)
    fixed = lambda i: (0, 0)
    qkv_shape = jax.ShapeDtypeStruct((n, 3 * d), F32)
    qkv_spec = pl.BlockSpec((tm, 3 * d), row)
    tail_specs = [pl.BlockSpec((1, d), fixed), pl.BlockSpec((d, 3 * d), fixed)]
    if moe is None:
        return None, pl.pallas_call(
            functools.partial(_qkv_kernel, combine=False),
            grid=(n // tm,),
            in_specs=[pl.BlockSpec((tm, d), row)] + tail_specs,
            out_specs=qkv_spec,
            out_shape=qkv_shape,
            compiler_params=_cparams(("arbitrary",)),
            name="norm_qkv",
        )(x, g, w_bf16)
    ys2, route = moe
    return pl.pallas_call(
        functools.partial(_qkv_kernel, combine=True),
        grid=(n // tm,),
        in_specs=_combine_specs(n, d, tm) + tail_specs,
        out_specs=[pl.BlockSpec((tm, d), row), qkv_spec],
        out_shape=[jax.ShapeDtypeStruct((n, d), F32), qkv_shape],
        compiler_params=_cparams(("arbitrary",)),
        name="combine_norm_qkv",
    )(x, ys2, ys2, route, g, w_bf16)


def _combine_kernel(x_ref, ya_ref, yb_ref, r_ref, o_ref):
    o_ref[...] = _moe_combine(x_ref, ya_ref, yb_ref, r_ref)


def _combine_call(x, ys2, route):
    n, d = x.shape
    tm = TOK_TILE
    return pl.pallas_call(
        _combine_kernel,
        grid=(n // tm,),
        in_specs=_combine_specs(n, d, tm),
        out_specs=pl.BlockSpec((tm, d), lambda i: (i, 0)),
        out_shape=jax.ShapeDtypeStruct((n, d), F32),
        compiler_params=_cparams(("arbitrary",)),
        name="combine_final",
    )(x, ys2, ys2, route)


def _proj_router_kernel(x_ref, o_ref, wo_ref, g_ref, wr_ref, br_ref, xo_ref, h_ref, r_ref):
    x = x_ref[...] + jnp.dot(o_ref[...], wo_ref[...], preferred_element_type=F32)
    xo_ref[...] = x
    h = _rms(x, g_ref[...])
    h_ref[...] = h
    logits = jnp.dot(h, wr_ref[...], preferred_element_type=F32,
                     precision=lax.Precision.HIGHEST) + br_ref[...]
    lane = lax.broadcasted_iota(jnp.int32, logits.shape, 1)
    last = ROUTER_LANES - 1

    def first_argmax(v, vmax):
        return jnp.min(jnp.where(v == vmax, lane, last), axis=-1, keepdims=True)

    gl = jnp.where(lane < N_GROUPS, logits, -jnp.inf)
    gmax = jnp.max(gl, axis=-1, keepdims=True)
    gsum = jnp.sum(jnp.exp(gl - gmax), axis=-1, keepdims=True)
    g_p = 1.0 / gsum
    g_idx = first_argmax(gl, gmax)
    e_lo = N_GROUPS + EXPERTS_PER_GROUP * g_idx
    el = jnp.where((lane >= e_lo) & (lane < e_lo + EXPERTS_PER_GROUP), logits, -jnp.inf)
    m1 = jnp.max(el, axis=-1, keepdims=True)
    i1 = first_argmax(el, m1)
    el2 = jnp.where(lane == i1, -jnp.inf, el)
    m2 = jnp.max(el2, axis=-1, keepdims=True)
    i2 = first_argmax(el2, m2)
    e2 = jnp.exp(m2 - m1)
    gate1 = g_p / (1.0 + e2)
    gate2 = g_p * e2 / (1.0 + e2)
    out = jnp.where(lane == 0, (i1 - N_GROUPS).astype(F32), 0.0)
    out = jnp.where(lane == 1, (i2 - N_GROUPS).astype(F32), out)
    out = jnp.where(lane == 2, gate1, out)
    out = jnp.where(lane == 3, gate2, out)
    r_ref[...] = out


def _proj_router_call(x, o_bf16, wo_bf16, g, wr, br):
    n, d = x.shape
    tm = TOK_TILE
    row = lambda i: (i, 0)
    fixed = lambda i: (0, 0)
    return pl.pallas_call(
        _proj_router_kernel,
        grid=(n // tm,),
        in_specs=[pl.BlockSpec((tm, d), row), pl.BlockSpec((tm, d), row), pl.BlockSpec((d, d), fixed),
                  pl.BlockSpec((1, d), fixed), pl.BlockSpec((d, ROUTER_LANES), fixed),
                  pl.BlockSpec((1, ROUTER_LANES), fixed)],
        out_specs=[pl.BlockSpec((tm, d), row), pl.BlockSpec((tm, d), row),
                   pl.BlockSpec((tm, ROUTER_LANES), row)],
        out_shape=[jax.ShapeDtypeStruct((n, d), F32), jax.ShapeDtypeStruct((n, d), F32),
                   jax.ShapeDtypeStruct((n, ROUTER_LANES), F32)],
        compiler_params=_cparams(("arbitrary",)),
        name="proj_router",
    )(x, o_bf16, wo_bf16, g, wr, br)


def _toeplitz(w, rows, cols):
    length = w.shape[-1]
    flat = jnp.tile(w, (1,) * (w.ndim - 1) + (rows,))[..., :rows * (length - 1)]
    return flat.reshape(w.shape[:-1] + (rows, length - 1))[..., :cols]


def _na_bias_table(rpb, rows):
    n_heads = rpb.shape[0]
    zeros = jnp.zeros(rpb.shape[:2] + (2 * GRID_W - 2 * WIN_COLS + 1,), rpb.dtype)
    w = jnp.concatenate([rpb[..., WIN_COLS - 1:], zeros, rpb[..., :WIN_COLS - 1]], axis=-1)
    toep = _toeplitz(w, GRID_W, GRID_W)
    toep = jnp.pad(toep, ((0, 0), (NA_GROUP_ROWS, NA_GROUP_ROWS), (0, 0), (0, 0)))
    groups = ((0, 0), (NA_GROUP_ROWS, 0), (rows - NA_GROUP_ROWS, rows - NA_WIN_ROWS))
    tiles, valids = [], []
    for r0, ws in groups:
        off = ws - r0 + WIN_ROWS - 1 + NA_GROUP_ROWS
        tiles.append(jnp.stack([toep[:, off - i:off - i + NA_WIN_ROWS] for i in range(NA_GROUP_ROWS)], axis=1))
        i = np.arange(NA_GROUP_ROWS)[:, None, None, None]
        c = np.arange(GRID_W)[None, :, None, None]
        kr = ws + np.arange(NA_WIN_ROWS)[None, None, :, None]
        kc = np.arange(GRID_W)[None, None, None, :]
        rs = np.clip(r0 + i - WIN_ROWS // 2, 0, rows - WIN_ROWS)
        cs = np.clip(c - WIN_COLS // 2, 0, GRID_W - WIN_COLS)
        valids.append((kr >= rs) & (kr < rs + WIN_ROWS) & (kc >= cs) & (kc < cs + WIN_COLS))
    tile = jnp.stack(tiles, axis=1).transpose(0, 1, 2, 4, 3, 5)
    tile = jnp.where(np.stack(valids)[None], tile, NEG_BIG)
    return tile.reshape(n_heads, len(groups), NA_GROUP_ROWS * GRID_W, NA_WIN_ROWS * GRID_W)


def _na_kernel(q_ref, k_ref, v_ref, qg_ref, kg_ref, bias_ref, o_ref, qn, kn, vb):
    t = q_ref.shape[1]
    rows = t // GRID_W
    n_groups = rows // NA_GROUP_ROWS
    gq = NA_GROUP_ROWS * GRID_W
    gk = NA_WIN_ROWS * GRID_W
    scale = NA_HEAD_DIM ** -0.5
    prep_rows = 512

    def prep(c, carry):
        sl = pl.ds(pl.multiple_of(c * prep_rows, prep_rows), prep_rows)
        qn[sl, :] = (_half_rms(q_ref[0, sl, :], qg_ref[...]) * scale).astype(BF16)
        kn[sl, :] = _half_rms(k_ref[0, sl, :], kg_ref[...]).astype(BF16)
        vb[sl, :] = v_ref[0, sl, :].astype(BF16)
        return carry

    lax.fori_loop(0, t // prep_rows, prep, 0)

    def group(g, carry):
        r0 = g * NA_GROUP_ROWS
        ws = jnp.clip(r0 - WIN_ROWS // 2, 0, rows - NA_WIN_ROWS)
        pat = jnp.where(g == 0, 0, jnp.where(g == n_groups - 1, 2, 1))
        qs = pl.ds(pl.multiple_of(r0 * GRID_W, GRID_W), gq)
        ks = pl.ds(pl.multiple_of(ws * GRID_W, GRID_W), gk)
        qg = qn[qs, :]
        kw = kn[ks, :]
        vw = vb[ks, :]
        lo = lax.broadcasted_iota(jnp.int32, qg.shape, 1) < NA_HEAD_DIM
        outs = []
        for hh in range(2):
            qh = jnp.where(lo if hh == 0 else jnp.logical_not(lo), qg, jnp.zeros_like(qg))
            s = lax.dot_general(qh, kw, (((1,), (1,)), ((), ())), preferred_element_type=F32)
            s = s + bias_ref[hh, pat]
            m = jnp.max(s, axis=-1, keepdims=True)
            p = jnp.exp(s - m)
            l = jnp.sum(p, axis=-1, keepdims=True)
            outs.append(jnp.dot(p.astype(BF16), vw, preferred_element_type=F32) / l)
        o_ref[0, qs, :] = jnp.where(lo, outs[0], outs[1]).astype(o_ref.dtype)
        return carry

    lax.fori_loop(0, n_groups, group, 0)


def _na_call(qkv, q_g, k_g, bias_tab):
    b, t, _ = qkv.shape
    n_pairs = NA_HEADS // 2
    g2 = lambda g: jnp.concatenate([g, g]).reshape(1, LANES)
    gq = NA_GROUP_ROWS * GRID_W
    gk = NA_WIN_ROWS * GRID_W
    blk = lambda off: pl.BlockSpec((1, t, LANES), lambda hp, bb, off=off: (bb, 0, off + hp))
    return pl.pallas_call(
        _na_kernel,
        grid=(n_pairs, b),
        in_specs=[blk(0), blk(n_pairs), blk(2 * n_pairs),
                  pl.BlockSpec((1, LANES), lambda hp, bb: (0, 0)),
                  pl.BlockSpec((1, LANES), lambda hp, bb: (0, 0)),
                  pl.BlockSpec((2, 3, gq, gk), lambda hp, bb: (hp, 0, 0, 0))],
        out_specs=pl.BlockSpec((1, t, LANES), lambda hp, bb: (bb, 0, hp)),
        out_shape=jax.ShapeDtypeStruct((b, t, D_MODEL), BF16),
        scratch_shapes=[pltpu.VMEM((t, LANES), BF16)] * 3,
        compiler_params=_cparams(("arbitrary", "arbitrary")),
        name="neighborhood_attention",
    )(qkv, qkv, qkv, g2(q_g), g2(k_g), bias_tab)


def _t5_bucket(rel):
    half = N_BUCKETS // 2
    max_exact = half // 2
    ret = jnp.where(rel > 0, half, 0)
    n = jnp.abs(rel)
    nf = jnp.maximum(n, 1).astype(jnp.float32)
    large = max_exact + (jnp.log(nf / max_exact) / math.log(MAX_DISTANCE / max_exact)
                         * (half - max_exact)).astype(jnp.int32)
    large = jnp.minimum(large, half - 1)
    return ret + jnp.where(n < max_exact, n, large)


_DIFF_DELTAS = (-2 * DIFF_TK, -DIFF_TK, -DIFF_TQ, 0, DIFF_TQ, 2 * DIFF_TK)


def _diff_bias_table(rel_bias):
    length = DIFF_TQ + DIFF_TK
    pos = np.arange(length)
    m = np.where(pos < DIFF_TK, pos, pos - length)
    rel = jnp.asarray((np.asarray(_DIFF_DELTAS)[:, None] + m[None, :]).astype(np.int32))
    w = rel_bias[_t5_bucket(rel)].astype(F32).transpose(2, 0, 1)
    return _toeplitz(w, DIFF_TQ, DIFF_TK)


def _diff_kernel(q_ref, k_ref, v_ref, qg_ref, kg_ref, sg_ref, lam_ref, bias_ref, o_ref, kn, vp, s_buf,
                 *, lambda_init):
    t = k_ref.shape[1]
    tq = DIFF_TQ
    tk = DIFF_TK
    n_chunks = t // tk
    qi = pl.program_id(2)
    scale = DIFF_HEAD_DIM ** -0.5

    @pl.when(qi == 0)
    def _():
        prep_rows = 512
        ones_col = (lax.broadcasted_iota(jnp.int32, (prep_rows, LANES), 1) == 0).astype(BF16)

        def prep(c, carry):
            sl = pl.ds(pl.multiple_of(c * prep_rows, prep_rows), prep_rows)
            kn[sl, :] = _half_rms(k_ref[0, sl, :], kg_ref[...]).astype(BF16)
            vp[sl, :LANES] = v_ref[0, sl, :].astype(BF16)
            vp[sl, LANES:] = ones_col
            return carry

        lax.fori_loop(0, t // prep_rows, prep, 0)

    q = _half_rms(q_ref[0], qg_ref[...]) * scale
    lo = lax.broadcasted_iota(jnp.int32, q.shape, 1) < DIFF_HEAD_DIM
    q2 = jnp.concatenate([jnp.where(lo, q, 0.0), jnp.where(lo, 0.0, q)], axis=0).astype(BF16)

    mpart = jnp.full((2 * tq, LANES), -jnp.inf, F32)
    for j in range(n_chunks):
        s = lax.dot_general(q2, kn[j * tk:(j + 1) * tk, :], (((1,), (1,)), ((), ())),
                            preferred_element_type=F32)
        var = jnp.clip(2 * j - qi + 3, 0, len(_DIFF_DELTAS) - 1)
        bt = bias_ref[0, var]
        s = s + jnp.concatenate([bt, bt], axis=0)
        s_buf[:, j * tk:(j + 1) * tk] = s
        for c in range(tk // LANES):
            mpart = jnp.maximum(mpart, s[:, c * LANES:(c + 1) * LANES])
    m = jnp.max(mpart, axis=-1, keepdims=True)

    acc = jnp.zeros((2 * tq, 2 * LANES), F32)
    for j in range(n_chunks):
        p = jnp.exp(s_buf[:, j * tk:(j + 1) * tk] - m).astype(BF16)
        acc = acc + jnp.dot(p, vp[j * tk:(j + 1) * tk, :], preferred_element_type=F32)

    lam = lam_ref[0, 0]
    o0 = acc[:tq, :LANES] / acc[:tq, LANES:LANES + 1]
    o1 = acc[tq:, :LANES] / acc[tq:, LANES:LANES + 1]
    o = o0 - lam * o1
    o = _rms(o, sg_ref[...]) * (1.0 - lambda_init)
    o_ref[0] = o.astype(o_ref.dtype)


def _diff_call(qkv, q_g, k_g, subln_g, lam, bias_tab, lambda_init):
    b, t, _ = qkv.shape
    g2 = lambda g: jnp.concatenate([g, g]).reshape(1, LANES)
    fixed = lambda bb, h, qi: (0, 0)
    return pl.pallas_call(
        functools.partial(_diff_kernel, lambda_init=lambda_init),
        grid=(b, DIFF_HEADS, t // DIFF_TQ),
        in_specs=[pl.BlockSpec((1, DIFF_TQ, LANES), lambda bb, h, qi: (bb, qi, h)),
                  pl.BlockSpec((1, t, LANES), lambda bb, h, qi: (bb, 0, DIFF_HEADS + h)),
                  pl.BlockSpec((1, t, LANES), lambda bb, h, qi: (bb, 0, 2 * DIFF_HEADS + h)),
                  pl.BlockSpec((1, LANES), fixed), pl.BlockSpec((1, LANES), fixed),
                  pl.BlockSpec((1, LANES), fixed),
                  pl.BlockSpec(memory_space=pltpu.SMEM),
                  pl.BlockSpec((1, len(_DIFF_DELTAS), DIFF_TQ, DIFF_TK), lambda bb, h, qi: (h, 0, 0, 0))],
        out_specs=pl.BlockSpec((1, DIFF_TQ, LANES), lambda bb, h, qi: (bb, qi, h)),
        out_shape=jax.ShapeDtypeStruct((b, t, D_MODEL), BF16),
        scratch_shapes=[pltpu.VMEM((t, LANES), BF16), pltpu.VMEM((t, 2 * LANES), BF16),
                        pltpu.VMEM((2 * DIFF_TQ, t), F32)],
        compiler_params=_cparams(("arbitrary", "arbitrary", "arbitrary")),
        name="differential_attention",
    )(qkv, qkv, qkv, g2(q_g), g2(k_g), subln_g.reshape(1, LANES), lam, bias_tab)


def _gather_kernel(idx_ref, src_ref, out_ref, sem):
    rows = out_ref.shape[0]
    base = (pl.program_id(1) * pl.num_programs(0) + pl.program_id(0)) * rows

    def row_copy(i, src_row):
        return pltpu.make_async_copy(src_ref.at[pl.ds(src_row, 1)], out_ref.at[pl.ds(i, 1)], sem)

    def issue(i, carry):
        row_copy(i, idx_ref[base + i]).start()
        return carry

    lax.fori_loop(0, rows, issue, 0)

    def wait(i, carry):
        row_copy(i, 0).wait()
        return carry

    lax.fori_loop(0, rows, wait, 0)


def _gather_rows(src, idx):
    n_col, m = idx.shape
    c = src.shape[1]
    return pl.pallas_call(
        _gather_kernel,
        grid_spec=pltpu.PrefetchScalarGridSpec(
            num_scalar_prefetch=1,
            grid=(m // MOE_BLOCK, n_col),
            in_specs=[pl.BlockSpec(memory_space=pl.ANY)],
            out_specs=pl.BlockSpec((MOE_BLOCK, c), lambda i, k, idx: (i, k)),
            scratch_shapes=[pltpu.SemaphoreType.DMA(())],
        ),
        out_shape=jax.ShapeDtypeStruct((m, n_col * c), src.dtype),
        compiler_params=_cparams(("arbitrary", "arbitrary")),
        name="gather_rows",
    )(idx.reshape(n_col * m), src)


SC_GATHER_WINDOW = 32


def _sc_gather_rows(src, idx):
    (m,) = idx.shape
    c = src.shape[1]
    w = SC_GATHER_WINDOW
    mesh = plsc.VectorSubcoreMesh(core_axis_name="c", subcore_axis_name="s")
    n_workers = mesh.num_cores * mesh.num_subcores
    per = m // n_workers
    assert per * n_workers == m and per % w == 0

    @functools.partial(pl.kernel, out_type=jax.ShapeDtypeStruct((m, c), src.dtype), mesh=mesh,
                       scratch_types=[pltpu.VMEM((per,), jnp.int32), pltpu.VMEM((w, c), src.dtype)],
                       name="sc_gather_rows")
    def gather(src_hbm, idx_hbm, out_hbm, idx_vmem, buf):
        base = (lax.axis_index("c") * mesh.num_subcores + lax.axis_index("s")) * per
        pltpu.sync_copy(idx_hbm.at[pl.ds(base, per)], idx_vmem)

        @pl.loop(0, per // w)
        def _(step):
            pltpu.sync_copy(src_hbm.at[idx_vmem.at[pl.ds(step * w, w)]], buf)
            pltpu.sync_copy(buf, out_hbm.at[pl.ds(base + step * w, w)])

    return gather(src, idx)


def _expert_kernel(be_ref, xs_ref, wg_ref, wu_ref, wd_ref, ys_ref, wg_b, wu_b, wd_b):
    i = pl.program_id(0)

    @pl.when((i == 0) | (be_ref[i] != be_ref[jnp.maximum(i - 1, 0)]))
    def _():
        wg_b[...] = wg_ref[0].astype(BF16)
        wu_b[...] = wu_ref[0].astype(BF16)
        wd_b[...] = wd_ref[0].astype(BF16)

    xb = xs_ref[...].astype(BF16)
    gate = jnp.dot(xb, wg_b[...], preferred_element_type=F32)
    up = jnp.dot(xb, wu_b[...], preferred_element_type=F32)
    hid = (gate * (1.0 / (1.0 + jnp.exp(-gate))) * up).astype(BF16)
    ys_ref[...] = jnp.dot(hid, wd_b[...], preferred_element_type=F32)


def _expert_call(xs, block_expert, w_gate, w_up, w_down):
    cap, d = xs.shape
    return pl.pallas_call(
        _expert_kernel,
        grid_spec=pltpu.PrefetchScalarGridSpec(
            num_scalar_prefetch=1,
            grid=(cap // MOE_BLOCK,),
            in_specs=[pl.BlockSpec((MOE_BLOCK, d), lambda i, be: (i, 0)),
                      pl.BlockSpec((1, d, D_EXPERT), lambda i, be: (be[i], 0, 0)),
                      pl.BlockSpec((1, d, D_EXPERT), lambda i, be: (be[i], 0, 0)),
                      pl.BlockSpec((1, D_EXPERT, d), lambda i, be: (be[i], 0, 0))],
            out_specs=pl.BlockSpec((MOE_BLOCK, d), lambda i, be: (i, 0)),
            scratch_shapes=[pltpu.VMEM((d, D_EXPERT), BF16), pltpu.VMEM((d, D_EXPERT), BF16),
                            pltpu.VMEM((D_EXPERT, d), BF16)],
        ),
        out_shape=jax.ShapeDtypeStruct((cap, d), F32),
        compiler_params=_cparams(("arbitrary",)),
        name="expert_swiglu",
    )(block_expert, xs, w_gate, w_up, w_down)


def _dispatch_plan(route, n_tok):
    n_assign = n_tok * TOP_K
    n_blocks = -(-n_assign // MOE_BLOCK) + N_EXPERTS
    cap = n_blocks * MOE_BLOCK
    flat_e = route[:, :TOP_K].astype(jnp.int32).reshape(n_assign)
    onehot = (flat_e[:, None] == jnp.arange(N_EXPERTS, dtype=jnp.int32)[None, :]).astype(jnp.int32)
    csum = jnp.cumsum(onehot, axis=0)
    counts = csum[-1]
    padded = (counts + MOE_BLOCK - 1) // MOE_BLOCK * MOE_BLOCK
    padded_end = jnp.cumsum(padded)
    padded_start = padded_end - padded
    dest = jnp.sum(onehot * (csum - 1 + padded_start[None, :]), axis=1)
    flat_tok = jnp.arange(n_assign, dtype=jnp.int32) // TOP_K
    slot_tok = (jnp.arange(cap, dtype=jnp.int32) % n_tok).at[dest].set(flat_tok)
    block_expert = jnp.minimum(
        jnp.searchsorted(padded_end, jnp.arange(n_blocks, dtype=jnp.int32) * MOE_BLOCK, side='right'),
        N_EXPERTS - 1).astype(jnp.int32)
    return slot_tok, dest.astype(jnp.int32), block_expert


def _moe_call(h, route, w_gate, w_up, w_down):
    n_tok, d = h.shape
    slot_tok, dest, block_expert = _dispatch_plan(route, n_tok)
    xs = _sc_gather_rows(h, slot_tok)
    ys = _expert_call(xs, block_expert, w_gate, w_up, w_down)
    return _sc_gather_rows(ys, dest.reshape(n_tok, TOP_K).T.reshape(-1))


def kernel(x, mix_norm_g, ffn_norm_g, na_w_qkv, na_w_o, na_q_g, na_k_g, na_rpb, diff_w_qkv, diff_w_o, diff_q_g, diff_k_g, diff_lq1, diff_lk1, diff_lq2, diff_lk2, diff_subln_g, rel_bias, router_group_w, router_group_b, router_expert_w, router_expert_b, moe_w_gate, moe_w_up, moe_w_down):
    b, t, d = x.shape
    depth = mix_norm_g.shape[0]
    n_tok = b * t
    xt = x.reshape(n_tok, d)
    diff_bias = _diff_bias_table(rel_bias)
    pad = ROUTER_LANES - N_GROUPS - N_EXPERTS
    moe = None
    for i in range(depth):
        j = i // 2
        is_na = i % 2 == 0
        w_qkv = (na_w_qkv if is_na else diff_w_qkv)[j].astype(BF16)
        w_o = (na_w_o if is_na else diff_w_o)[j].astype(BF16)
        x_new, qkv = _qkv_call(xt, mix_norm_g[i].reshape(1, d), w_qkv, moe)
        if x_new is not None:
            xt = x_new
        qkv = qkv.reshape(b, t, 3 * d)
        if is_na:
            o = _na_call(qkv, na_q_g[j], na_k_g[j], _na_bias_table(na_rpb[j], t // GRID_W))
        else:
            lambda_init = 0.8 - 0.6 * math.exp(-0.3 * i)
            lam = (jnp.exp(jnp.sum(diff_lq1[j] * diff_lk1[j])) - jnp.exp(jnp.sum(diff_lq2[j] * diff_lk2[j]))
                   + lambda_init).reshape(1, 1)
            o = _diff_call(qkv, diff_q_g[j], diff_k_g[j], diff_subln_g[j], lam, diff_bias, lambda_init)
        wr = jnp.pad(jnp.concatenate([router_group_w[i], router_expert_w[i]], axis=1), ((0, 0), (0, pad)))
        br = jnp.pad(jnp.concatenate([router_group_b[i], router_expert_b[i]]), (0, pad)).reshape(1, ROUTER_LANES)
        xt, h, route = _proj_router_call(xt, o.reshape(n_tok, d), w_o, ffn_norm_g[i].reshape(1, d), wr, br)
        moe = (_moe_call(h, route, moe_w_gate[i], moe_w_up[i], moe_w_down[i]), route)
    return _combine_call(xt, *moe).reshape(b, t, d)
```
